```python
import math
import jax, jax.numpy as jnp
from jax import lax
import numpy as np

D_MODEL = 4096
BATCH = 1
SEQ = 16384
DEPTH = 2

N_MIXERS = 2
N_CONV_LAYERS = (DEPTH + 1) // 2
N_ATTN_LAYERS = DEPTH // 2

CONV_WIDTH = 31

HEAD_DIM = 128
N_HEADS = D_MODEL // HEAD_DIM
N_KV_HEADS = 8
GROUP = N_HEADS // N_KV_HEADS
WINDOW = 128
BLOCK = 128
QKV_DIM = (N_HEADS + 2 * N_KV_HEADS) * HEAD_DIM
ATTN_SCALE = 1.0 / math.sqrt(HEAD_DIM)

NUM_BUCKETS = 32
MAX_EXACT = NUM_BUCKETS // 2
MAX_DISTANCE = 128

D_FF = 11008
FFN_CONV_WIDTH = 3

EPS = 1e-6

kernel_name = "interleaved_conformer_conv_swa_sink_hybrid"


def rms_norm(x, g):
    xf = x.astype(jnp.float32)
    y = xf * lax.rsqrt(jnp.mean(xf * xf, axis=-1, keepdims=True) + EPS)
    return (y * g.astype(jnp.float32)).astype(x.dtype)


def layer_norm(x, g, b):
    xf = x.astype(jnp.float32)
    mu = jnp.mean(xf, axis=-1, keepdims=True)
    xc = xf - mu
    y = xc * lax.rsqrt(jnp.mean(xc * xc, axis=-1, keepdims=True) + EPS)
    return (y * g.astype(jnp.float32) + b.astype(jnp.float32)).astype(x.dtype)


def causal_depthwise_conv(x, w, b):
    width = w.shape[0]
    y = lax.conv_general_dilated(
        x, w[:, None, :].astype(x.dtype), window_strides=(1,),
        padding=[(width - 1, 0)], dimension_numbers=("NWC", "WIO", "NWC"),
        feature_group_count=x.shape[-1])
    return y + b.astype(x.dtype)


def t5_causal_bucket(dist):
    n = jnp.maximum(dist, 0)
    is_small = n < MAX_EXACT
    nf = jnp.maximum(n, 1).astype(jnp.float32)
    large = MAX_EXACT + (jnp.log(nf / MAX_EXACT) / math.log(MAX_DISTANCE / MAX_EXACT)
                         * (NUM_BUCKETS - MAX_EXACT)).astype(jnp.int32)
    large = jnp.minimum(large, NUM_BUCKETS - 1)
    return jnp.where(is_small, n, large)


def conformer_conv(h, pw1_w, pw1_b, dw_w, dw_b, ln_g, ln_b, pw2_w, pw2_b):
    a, g = jnp.split(h @ pw1_w + pw1_b, 2, axis=-1)
    u = a * jax.nn.sigmoid(g)
    u = causal_depthwise_conv(u, dw_w, dw_b)
    u = jax.nn.silu(layer_norm(u, ln_g, ln_b))
    return u @ pw2_w + pw2_b


def sliding_window_attention(h, w_qkv, q_g, k_g, sinks, w_o, rel_bias):
    B, T, _ = h.shape
    nb = T // BLOCK
    qkv = h @ w_qkv
    q, k, v = jnp.split(qkv, [N_HEADS * HEAD_DIM, (N_HEADS + N_KV_HEADS) * HEAD_DIM], axis=-1)
    q = rms_norm(q.reshape(B, T, N_HEADS, HEAD_DIM), q_g)
    k = rms_norm(k.reshape(B, T, N_KV_HEADS, HEAD_DIM), k_g)
    v = v.reshape(B, T, N_KV_HEADS, HEAD_DIM)
    q = q.reshape(B, nb, BLOCK, N_KV_HEADS, GROUP, HEAD_DIM)

    def band(t):
        t = t.reshape(B, nb, BLOCK, N_KV_HEADS, HEAD_DIM)
        prev = jnp.concatenate([jnp.zeros_like(t[:, :1]), t[:, :-1]], axis=1)
        return jnp.concatenate([prev, t], axis=2)

    kb, vb = band(k), band(v)
    qi = jnp.arange(BLOCK, dtype=jnp.int32)[:, None]
    kj = jnp.arange(2 * BLOCK, dtype=jnp.int32)[None, :]
    dist = qi - kj + BLOCK
    in_window = (dist >= 0) & (dist < WINDOW)
    kpos = jnp.arange(nb, dtype=jnp.int32)[:, None] * BLOCK - BLOCK + kj
    mask = in_window[None] & (kpos >= 0)[:, None, :]

    bias = rel_bias.astype(jnp.float32)[t5_causal_bucket(dist)]
    bias = bias.transpose(2, 0, 1).reshape(N_KV_HEADS, GROUP, BLOCK, 2 * BLOCK)

    s = jnp.einsum("bnqkgd,bnskd->bnkgqs", q, kb).astype(jnp.float32) * ATTN_SCALE + bias
    s = jnp.where(mask[None, :, None, None], s, -jnp.inf)
    sink = sinks.astype(jnp.float32).reshape(N_KV_HEADS, GROUP)[:, :, None, None]
    m = jnp.maximum(jnp.max(s, axis=-1, keepdims=True), sink)
    p = jnp.exp(s - m)
    denom = jnp.sum(p, axis=-1, keepdims=True) + jnp.exp(sink - m)
    p = (p / denom).astype(vb.dtype)
    o = jnp.einsum("bnkgqs,bnskd->bnqkgd", p, vb).reshape(B, T, N_HEADS * HEAD_DIM)
    return o @ w_o


def conv_gated_ffn(h, w_in, dw_w, dw_b, w_out):
    gate, val = jnp.split(h @ w_in, 2, axis=-1)
    gate = causal_depthwise_conv(gate, dw_w, dw_b)
    return (jax.nn.silu(gate) * val) @ w_out


def setup_inputs(seed: int = 0) -> dict:
    key = jax.random.key(seed)
    ks = jax.random.split(key, 24)
    f32 = jnp.float32

    def nrm(k, shape, scale):
        return jax.random.normal(k, shape, f32) * scale

    NC, NA = N_CONV_LAYERS, N_ATTN_LAYERS
    D, F = D_MODEL, D_FF
    return {
        "x": nrm(ks[0], (BATCH, SEQ, D), 1.0),
        "mix_norm_g": 1.0 + nrm(ks[1], (DEPTH, D), 0.1),
        "conv_pw1_w": nrm(ks[2], (NC, D, 2 * D), D ** -0.5),
        "conv_pw1_b": nrm(ks[3], (NC, 2 * D), 0.02),
        "conv_dw_w": nrm(ks[4], (NC, CONV_WIDTH, D), CONV_WIDTH ** -0.5),
        "conv_dw_b": nrm(ks[5], (NC, D), 0.02),
        "conv_ln_g": 1.0 + nrm(ks[6], (NC, D), 0.1),
        "conv_ln_b": nrm(ks[7], (NC, D), 0.02),
        "conv_pw2_w": nrm(ks[8], (NC, D, D), D ** -0.5),
        "conv_pw2_b": nrm(ks[9], (NC, D), 0.02),
        "attn_w_qkv": nrm(ks[10], (NA, D, QKV_DIM), D ** -0.5),
        "attn_q_norm_g": 1.0 + nrm(ks[11], (NA, HEAD_DIM), 0.1),
        "attn_k_norm_g": 1.0 + nrm(ks[12], (NA, HEAD_DIM), 0.1),
        "attn_sinks": nrm(ks[13], (NA, N_HEADS), 1.0),
        "attn_w_o": nrm(ks[14], (NA, N_HEADS * HEAD_DIM, D), (N_HEADS * HEAD_DIM) ** -0.5),
        "rel_bias": nrm(ks[15], (NUM_BUCKETS, N_HEADS), 0.5),
        "ffn_norm_g": 1.0 + nrm(ks[16], (DEPTH, D), 0.1),
        "ffn_w_in": nrm(ks[17], (DEPTH, D, 2 * F), D ** -0.5),
        "ffn_dw_w": nrm(ks[18], (DEPTH, FFN_CONV_WIDTH, F), FFN_CONV_WIDTH ** -0.5),
        "ffn_dw_b": nrm(ks[19], (DEPTH, F), 0.02),
        "ffn_w_out": nrm(ks[20], (DEPTH, F, D), F ** -0.5),
    }


def reference(x, mix_norm_g, conv_pw1_w, conv_pw1_b, conv_dw_w, conv_dw_b, conv_ln_g,
              conv_ln_b, conv_pw2_w, conv_pw2_b, attn_w_qkv, attn_q_norm_g, attn_k_norm_g,
              attn_sinks, attn_w_o, rel_bias, ffn_norm_g, ffn_w_in, ffn_dw_w, ffn_dw_b,
              ffn_w_out):
    for i in range(DEPTH):
        h = rms_norm(x, mix_norm_g[i])
        j = i // N_MIXERS
        if i % N_MIXERS == 0:
            y = conformer_conv(h, conv_pw1_w[j], conv_pw1_b[j], conv_dw_w[j], conv_dw_b[j],
                               conv_ln_g[j], conv_ln_b[j], conv_pw2_w[j], conv_pw2_b[j])
        else:
            y = sliding_window_attention(h, attn_w_qkv[j], attn_q_norm_g[j], attn_k_norm_g[j],
                                         attn_sinks[j], attn_w_o[j], rel_bias)
        x = x + y
        h = rms_norm(x, ffn_norm_g[i])
        x = x + conv_gated_ffn(h, ffn_w_in[i], ffn_dw_w[i], ffn_dw_b[i], ffn_w_out[i])
    return x
```

```python
import functools
import math

import jax
import jax.numpy as jnp
import numpy as np
from jax import lax
from jax.experimental import pallas as pl
from jax.experimental.pallas import tpu as pltpu

F32 = jnp.float32
BF16 = jnp.bfloat16

HEAD_DIM = 128
GROUP = 4
WINDOW = 128
BLOCK = 128
NUM_BUCKETS = 32
MAX_EXACT = NUM_BUCKETS // 2
MAX_DISTANCE = 128
CONV_WIDTH = 31
CONV_HALO = 32
FFN_CONV_WIDTH = 3
FFN_HALO = 16
EPS = 1e-6
ATTN_SCALE = 1.0 / math.sqrt(HEAD_DIM)

V7X_VMEM_BYTES = 64 * 1024 * 1024
VMEM_LIMIT_CAP = V7X_VMEM_BYTES - 6 * 1024 * 1024


def _params(semantics, vmem_bytes):
    limit = int(min(max(vmem_bytes * 4 // 3, 16 * 1024 * 1024), VMEM_LIMIT_CAP))
    return pltpu.CompilerParams(dimension_semantics=semantics, vmem_limit_bytes=limit)


def _nbytes(shape, dtype):
    return int(np.prod(shape)) * jnp.dtype(dtype).itemsize


def _tile(n, pref):
    t = min(n, pref)
    assert n % t == 0, (n, t)
    return t


def _rmsnorm_body(x_ref, g_ref, o_ref):
    x = x_ref[...]
    ms = jnp.mean(x * x, axis=-1, keepdims=True)
    o_ref[...] = ((x * lax.rsqrt(ms + EPS)) * g_ref[...]).astype(o_ref.dtype)


def rmsnorm(x, g):
    t, d = x.shape
    tm = _tile(t, 512)
    vmem = 2 * _nbytes((tm, d), F32) + 2 * _nbytes((tm, d), BF16) + 2 * _nbytes((tm, d), F32)
    return pl.pallas_call(
        _rmsnorm_body,
        grid=(t // tm,),
        in_specs=[pl.BlockSpec((tm, d), lambda i: (i, 0)),
                  pl.BlockSpec((1, d), lambda i: (0, 0))],
        out_specs=pl.BlockSpec((tm, d), lambda i: (i, 0)),
        out_shape=jax.ShapeDtypeStruct((t, d), BF16),
        compiler_params=_params(("parallel",), vmem),
        name="rmsnorm",
    )(x, g.reshape(1, d))


def _glu_body(h_ref, wa_ref, wg_ref, ba_ref, bg_ref, o_ref):
    h = h_ref[...]
    a = jnp.dot(h, wa_ref[...], preferred_element_type=F32) + ba_ref[...]
    g = jnp.dot(h, wg_ref[...], preferred_element_type=F32) + bg_ref[...]
    o_ref[...] = a * jax.nn.sigmoid(g)


def pw1_glu(h, w, b):
    t, d = h.shape
    tm = _tile(t, 1024)
    tn = _tile(d, 512)
    half = d // tn
    b2 = b.reshape(1, 2 * d)
    vmem = (2 * _nbytes((tm, d), BF16) + 4 * _nbytes((d, tn), BF16) + 2 * _nbytes((tm, tn), F32)
            + 4 * _nbytes((tm, tn), F32))
    return pl.pallas_call(
        _glu_body,
        grid=(t // tm, d // tn),
        in_specs=[pl.BlockSpec((tm, d), lambda i, j: (i, 0)),
                  pl.BlockSpec((d, tn), lambda i, j: (0, j)),
                  pl.BlockSpec((d, tn), lambda i, j: (0, j + half)),
                  pl.BlockSpec((1, tn), lambda i, j: (0, j)),
                  pl.BlockSpec((1, tn), lambda i, j: (0, j + half))],
        out_specs=pl.BlockSpec((tm, tn), lambda i, j: (i, j)),
        out_shape=jax.ShapeDtypeStruct((t, d), F32),
        compiler_params=_params(("parallel", "arbitrary"), vmem),
        name="pw1_glu",
    )(h, w, w, b2, b2)


def _conv_ln_body(ucur_ref, uprev_ref, w_ref, b_ref, g_ref, beta_ref, o_ref, s_ref, c_ref, *, tc, cw):
    i = pl.program_id(0)
    prev = uprev_ref[...]
    s_ref[0:CONV_HALO, :] = jnp.where(i > 0, prev, jnp.zeros_like(prev))
    s_ref[CONV_HALO:, :] = ucur_ref[...]
    d = s_ref.shape[1]
    first = CONV_HALO - (CONV_WIDTH - 1)

    def chunk(c, carry):
        off = pl.multiple_of(c * cw, cw)
        acc = s_ref[pl.ds(first, tc), pl.ds(off, cw)] * w_ref[pl.ds(0, 1), pl.ds(off, cw)]
        for k in range(1, CONV_WIDTH):
            acc = acc + s_ref[pl.ds(first + k, tc), pl.ds(off, cw)] * w_ref[pl.ds(k, 1), pl.ds(off, cw)]
        c_ref[:, pl.ds(off, cw)] = acc + b_ref[:, pl.ds(off, cw)]
        return carry

    lax.fori_loop(0, d // cw, chunk, 0)
    y = c_ref[...]
    mu = jnp.mean(y, axis=-1, keepdims=True)
    yc = y - mu
    var = jnp.mean(yc * yc, axis=-1, keepdims=True)
    z = (yc * lax.rsqrt(var + EPS)) * g_ref[...] + beta_ref[...]
    o_ref[...] = (z * jax.nn.sigmoid(z)).astype(o_ref.dtype)


def conv_ln_swish(u, w, b, ln_g, ln_b):
    t, d = u.shape
    tc = _tile(t, 256)
    cw = 128
    steps = tc // CONV_HALO
    vmem = (2 * _nbytes((tc, d), F32) + 2 * _nbytes((CONV_HALO, d), F32) + 2 * _nbytes((CONV_WIDTH, d), F32)
            + 2 * _nbytes((tc, d), BF16) + _nbytes((tc + CONV_HALO, d), F32) + 4 * _nbytes((tc, d), F32))
    return pl.pallas_call(
        functools.partial(_conv_ln_body, tc=tc, cw=cw),
        grid=(t // tc,),
        in_specs=[pl.BlockSpec((tc, d), lambda i: (i, 0)),
                  pl.BlockSpec((CONV_HALO, d), lambda i: (jnp.maximum(i * steps - 1, 0), 0)),
                  pl.BlockSpec((CONV_WIDTH, d), lambda i: (0, 0)),
                  pl.BlockSpec((1, d), lambda i: (0, 0)),
                  pl.BlockSpec((1, d), lambda i: (0, 0)),
                  pl.BlockSpec((1, d), lambda i: (0, 0))],
        out_specs=pl.BlockSpec((tc, d), lambda i: (i, 0)),
        out_shape=jax.ShapeDtypeStruct((t, d), BF16),
        scratch_shapes=[pltpu.VMEM((tc + CONV_HALO, d), F32), pltpu.VMEM((tc, d), F32)],
        compiler_params=_params(("parallel",), vmem),
        name="conv_ln_swish",
    )(u, u, w, b.reshape(1, d), ln_g.reshape(1, d), ln_b.reshape(1, d))


def _proj_res_body(a_ref, w_ref, b_ref, x_ref, o_ref):
    y = jnp.dot(a_ref[...], w_ref[...], preferred_element_type=F32)
    o_ref[...] = x_ref[...] + (y + b_ref[...])


def _proj_res_nobias_body(a_ref, w_ref, x_ref, o_ref):
    y = jnp.dot(a_ref[...], w_ref[...], preferred_element_type=F32)
    o_ref[...] = x_ref[...] + y


def proj_residual(a, w, x, b=None, *, tm_pref, tn_pref, name):
    t, k = a.shape
    n = w.shape[1]
    tm = _tile(t, tm_pref)
    tn = _tile(n, tn_pref)
    vmem = (2 * _nbytes((tm, k), BF16) + 2 * _nbytes((k, tn), BF16) + 4 * _nbytes((tm, tn), F32)
            + 2 * _nbytes((tm, tn), F32))
    a_spec = pl.BlockSpec((tm, k), lambda i, j: (i, 0))
    w_spec = pl.BlockSpec((k, tn), lambda i, j: (0, j))
    x_spec = pl.BlockSpec((tm, tn), lambda i, j: (i, j))
    if b is None:
        body, in_specs, args = _proj_res_nobias_body, [a_spec, w_spec, x_spec], (a, w, x)
    else:
        b_spec = pl.BlockSpec((1, tn), lambda i, j: (0, j))
        body, in_specs, args = _proj_res_body, [a_spec, w_spec, b_spec, x_spec], (a, w, b.reshape(1, n), x)
    return pl.pallas_call(
        body,
        grid=(t // tm, n // tn),
        in_specs=in_specs,
        out_specs=x_spec,
        out_shape=jax.ShapeDtypeStruct((t, n), F32),
        compiler_params=_params(("parallel", "arbitrary"), vmem),
        name=name,
    )(*args)


def _ffn_in_body(h_ref, halo_ref, wg_ref, wv_ref, cw_ref, cb_ref, o_ref, a_ref, g_ref, *, tm):
    i = pl.program_id(0)
    j = pl.program_id(1)

    @pl.when(j == 0)
    def _():
        halo = halo_ref[...]
        a_ref[0:FFN_HALO, :] = jnp.where(i > 0, halo, jnp.zeros_like(halo))
        a_ref[FFN_HALO:, :] = h_ref[...]

    g_ref[...] = jnp.dot(a_ref[...], wg_ref[...], preferred_element_type=F32)
    val = jnp.dot(a_ref[FFN_HALO:, :], wv_ref[...], preferred_element_type=F32)
    first = FFN_HALO - (FFN_CONV_WIDTH - 1)
    conv = g_ref[pl.ds(first, tm), :] * cw_ref[0:1, :]
    for k in range(1, FFN_CONV_WIDTH):
        conv = conv + g_ref[pl.ds(first + k, tm), :] * cw_ref[k:k + 1, :]
    conv = conv + cb_ref[...]
    o_ref[...] = ((conv * jax.nn.sigmoid(conv)) * val).astype(o_ref.dtype)


def ffn_in(h, w_in, dw_w, dw_b):
    t, d = h.shape
    f = w_in.shape[1] // 2
    tm = _tile(t, 1024)
    tn = _tile(f, 256)
    half = f // tn
    steps = tm // FFN_HALO
    vmem = (2 * _nbytes((tm, d), BF16) + 2 * _nbytes((FFN_HALO, d), BF16) + 4 * _nbytes((d, tn), BF16)
            + 2 * _nbytes((tm, tn), BF16) + _nbytes((tm + FFN_HALO, d), BF16)
            + 5 * _nbytes((tm + FFN_HALO, tn), F32))
    return pl.pallas_call(
        functools.partial(_ffn_in_body, tm=tm),
        grid=(t // tm, f // tn),
        in_specs=[pl.BlockSpec((tm, d), lambda i, j: (i, 0)),
                  pl.BlockSpec((FFN_HALO, d), lambda i, j: (jnp.maximum(i * steps - 1, 0), 0)),
                  pl.BlockSpec((d, tn), lambda i, j: (0, j)),
                  pl.BlockSpec((d, tn), lambda i, j: (0, j + half)),
                  pl.BlockSpec((FFN_CONV_WIDTH, tn), lambda i, j: (0, j)),
                  pl.BlockSpec((1, tn), lambda i, j: (0, j))],
        out_specs=pl.BlockSpec((tm, tn), lambda i, j: (i, j)),
        out_shape=jax.ShapeDtypeStruct((t, f), BF16),
        scratch_shapes=[pltpu.VMEM((tm + FFN_HALO, d), BF16), pltpu.VMEM((tm + FFN_HALO, tn), F32)],
        compiler_params=_params(("parallel", "arbitrary"), vmem),
        name="ffn_in",
    )(h, h, w_in, w_in, dw_w, dw_b.reshape(1, f))


def _qkv_body(h_ref, w_ref, qg_ref, kg_ref, o_ref, *, q_blocks, qk_blocks, heads_per_block):
    j = pl.program_id(1)
    acc = jnp.dot(h_ref[...], w_ref[...], preferred_element_type=F32)

    @pl.when(j < qk_blocks)
    def _():
        g = jnp.where(j < q_blocks, qg_ref[...], kg_ref[...])
        for hh in range(heads_per_block):
            blk = acc[:, hh * HEAD_DIM:(hh + 1) * HEAD_DIM]
            ms = jnp.mean(blk * blk, axis=-1, keepdims=True)
            o_ref[:, hh * HEAD_DIM:(hh + 1) * HEAD_DIM] = ((blk * lax.rsqrt(ms + EPS)) * g).astype(o_ref.dtype)

    @pl.when(j >= qk_blocks)
    def _():
        o_ref[...] = acc.astype(o_ref.dtype)


def qkv_proj(h, w, q_g, k_g, n_heads, n_kv):
    t, d = h.shape
    n = w.shape[1]
    tm = _tile(t, 1024)
    tn = _tile(n_kv * HEAD_DIM, 512)
    hpb = tn // HEAD_DIM
    q_blocks = n_heads // hpb
    qk_blocks = (n_heads + n_kv) // hpb
    vmem = (2 * _nbytes((tm, d), BF16) + 2 * _nbytes((d, tn), BF16) + 2 * _nbytes((tm, tn), BF16)
            + 4 * _nbytes((tm, tn), F32))
    return pl.pallas_call(
        functools.partial(_qkv_body, q_blocks=q_blocks, qk_blocks=qk_blocks, heads_per_block=hpb),
        grid=(t // tm, n // tn),
        in_specs=[pl.BlockSpec((tm, d), lambda i, j: (i, 0)),
                  pl.BlockSpec((d, tn), lambda i, j: (0, j)),
                  pl.BlockSpec((1, HEAD_DIM), lambda i, j: (0, 0)),
                  pl.BlockSpec((1, HEAD_DIM), lambda i, j: (0, 0))],
        out_specs=pl.BlockSpec((tm, tn), lambda i, j: (i, j)),
        out_shape=jax.ShapeDtypeStruct((t, n), BF16),
        compiler_params=_params(("parallel", "arbitrary"), vmem),
        name="qkv_proj",
    )(h, w, q_g.reshape(1, HEAD_DIM), k_g.reshape(1, HEAD_DIM))


def _bucket_index():
    qi = np.arange(BLOCK)[:, None]
    kj = np.arange(2 * BLOCK)[None, :]
    dist = qi - kj + BLOCK
    n = np.maximum(dist, 0)
    nf = np.maximum(n, 1).astype(np.float64)
    large = MAX_EXACT + (np.log(nf / MAX_EXACT) / math.log(MAX_DISTANCE / MAX_EXACT)
                         * (NUM_BUCKETS - MAX_EXACT)).astype(np.int64)
    large = np.minimum(large, NUM_BUCKETS - 1)
    bucket = np.where(n < MAX_EXACT, n, large)
    in_window = (dist >= 0) & (dist < WINDOW)
    return np.where(in_window, bucket, -1).astype(np.int32)


def _bias_body(rb_ref, idx_ref, o_ref):
    h = pl.program_id(0)
    idx = idx_ref[...]
    acc = jnp.full(idx.shape, -jnp.inf, F32)
    for b in range(NUM_BUCKETS):
        acc = jnp.where(idx == b, rb_ref[b, h], acc)
    o_ref[0] = acc


def bias_table(rel_bias):
    n_heads = rel_bias.shape[1]
    idx = jnp.asarray(_bucket_index())
    return pl.pallas_call(
        _bias_body,
        grid=(n_heads,),
        in_specs=[pl.BlockSpec(memory_space=pltpu.SMEM),
                  pl.BlockSpec((BLOCK, 2 * BLOCK), lambda h: (0, 0))],
        out_specs=pl.BlockSpec((1, BLOCK, 2 * BLOCK), lambda h: (h, 0, 0)),
        out_shape=jax.ShapeDtypeStruct((n_heads, BLOCK, 2 * BLOCK), F32),
        compiler_params=_params(("arbitrary",), 0),
        name="bias_table",
    )(rel_bias, idx)


def _attn_body(sink_ref, q_ref, kc_ref, kp_ref, vc_ref, vp_ref, bias_ref, o_ref, *, n_kv):
    n = pl.program_id(0)
    rows = GROUP * BLOCK
    col = lax.broadcasted_iota(jnp.int32, (rows, 2 * BLOCK), 1)
    no_prev = jnp.logical_and(n == 0, col < BLOCK)
    for kv in range(n_kv):
        ksl = slice(kv * HEAD_DIM, (kv + 1) * HEAD_DIM)
        k = jnp.concatenate([kp_ref[:, ksl], kc_ref[:, ksl]], axis=0)
        v = jnp.concatenate([vp_ref[:, ksl], vc_ref[:, ksl]], axis=0)
        heads = [kv * GROUP + g for g in range(GROUP)]
        q = jnp.concatenate([q_ref[:, h * HEAD_DIM:(h + 1) * HEAD_DIM] for h in heads], axis=0)
        s = lax.dot_general(q, k, (((1,), (1,)), ((), ())), preferred_element_type=F32)
        bias = bias_ref[kv * GROUP:(kv + 1) * GROUP].reshape(rows, 2 * BLOCK)
        s = s * ATTN_SCALE + bias
        s = jnp.where(no_prev, -jnp.inf, s)
        sink = jnp.concatenate([jnp.full((BLOCK, 1), sink_ref[h], F32) for h in heads], axis=0)
        m = jnp.maximum(jnp.max(s, axis=-1, keepdims=True), sink)
        p = jnp.exp(s - m)
        denom = jnp.sum(p, axis=-1, keepdims=True) + jnp.exp(sink - m)
        p = (p / denom).astype(v.dtype)
        o = jnp.dot(p, v, preferred_element_type=F32)
        for g, h in enumerate(heads):
            o_ref[:, h * HEAD_DIM:(h + 1) * HEAD_DIM] = o[g * BLOCK:(g + 1) * BLOCK].astype(o_ref.dtype)


def attention(qkv, bias, sinks, n_heads, n_kv):
    t = qkv.shape[0]
    dq = n_heads * HEAD_DIM
    dkv = n_kv * HEAD_DIM
    k_blk = dq // dkv
    v_blk = k_blk + 1
    prev = lambda n: jnp.maximum(n - 1, 0)
    vmem = (2 * _nbytes((BLOCK, dq), BF16) * 2 + 8 * _nbytes((BLOCK, dkv), BF16)
            + 2 * _nbytes(bias.shape, F32) + 16 * _nbytes((GROUP * BLOCK, 2 * BLOCK), F32))
    return pl.pallas_call(
        functools.partial(_attn_body, n_kv=n_kv),
        grid=(t // BLOCK,),
        in_specs=[pl.BlockSpec(memory_space=pltpu.SMEM),
                  pl.BlockSpec((BLOCK, dq), lambda n: (n, 0)),
                  pl.BlockSpec((BLOCK, dkv), lambda n: (n, k_blk)),
                  pl.BlockSpec((BLOCK, dkv), lambda n: (prev(n), k_blk)),
                  pl.BlockSpec((BLOCK, dkv), lambda n: (n, v_blk)),
                  pl.BlockSpec((BLOCK, dkv), lambda n: (prev(n), v_blk)),
                  pl.BlockSpec(bias.shape, lambda n: (0, 0, 0))],
        out_specs=pl.BlockSpec((BLOCK, dq), lambda n: (n, 0)),
        out_shape=jax.ShapeDtypeStruct((t, dq), BF16),
        compiler_params=_params(("parallel",), vmem),
        name="swa_attention",
    )(sinks, qkv, qkv, qkv, qkv, qkv, bias)


def _ffn_block(x, norm_g, w_in, dw_w, dw_b, w_out, name):
    h = rmsnorm(x, norm_g)
    act = ffn_in(h, w_in.astype(BF16), dw_w, dw_b)
    return proj_residual(act, w_out.astype(BF16), x, tm_pref=512, tn_pref=512, name=name)


def kernel(x, mix_norm_g, conv_pw1_w, conv_pw1_b, conv_dw_w, conv_dw_b, conv_ln_g, conv_ln_b, conv_pw2_w, conv_pw2_b, attn_w_qkv, attn_q_norm_g, attn_k_norm_g, attn_sinks, attn_w_o, rel_bias, ffn_norm_g, ffn_w_in, ffn_dw_w, ffn_dw_b, ffn_w_out):
    batch, t, d = x.shape
    assert batch == 1
    n_heads = attn_sinks.shape[1]
    n_kv = n_heads // GROUP
    assert attn_w_qkv.shape[2] == (n_heads + 2 * n_kv) * HEAD_DIM and t % BLOCK == 0
    x = x.reshape(t, d)

    h = rmsnorm(x, mix_norm_g[0])
    u = pw1_glu(h, conv_pw1_w[0].astype(BF16), conv_pw1_b[0])
    v = conv_ln_swish(u, conv_dw_w[0], conv_dw_b[0], conv_ln_g[0], conv_ln_b[0])
    x = proj_residual(v, conv_pw2_w[0].astype(BF16), x, conv_pw2_b[0], tm_pref=1024, tn_pref=512, name="pw2_res")
    x = _ffn_block(x, ffn_norm_g[0], ffn_w_in[0], ffn_dw_w[0], ffn_dw_b[0], ffn_w_out[0], "ffn_out0")

    h = rmsnorm(x, mix_norm_g[1])
    qkv = qkv_proj(h, attn_w_qkv[0].astype(BF16), attn_q_norm_g[0], attn_k_norm_g[0], n_heads, n_kv)
    o = attention(qkv, bias_table(rel_bias), attn_sinks[0], n_heads, n_kv)
    x = proj_residual(o, attn_w_o[0].astype(BF16), x, tm_pref=1024, tn_pref=512, name="wo_res")
    x = _ffn_block(x, ffn_norm_g[1], ffn_w_in[1], ffn_dw_w[1], ffn_dw_b[1], ffn_w_out[1], "ffn_out1")
    return x.reshape(batch, t, d)
```

```python
import functools
import math

import jax
import jax.numpy as jnp
import numpy as np
from jax import lax
from jax.experimental import pallas as pl
from jax.experimental.pallas import tpu as pltpu

F32 = jnp.float32
BF16 = jnp.bfloat16

HEAD_DIM = 128
GROUP = 4
WINDOW = 128
BLOCK = 128
NUM_BUCKETS = 32
MAX_EXACT = NUM_BUCKETS // 2
MAX_DISTANCE = 128
CONV_WIDTH = 31
CONV_HALO = 32
FFN_CONV_WIDTH = 3
EPS = 1e-6
ATTN_SCALE = 1.0 / math.sqrt(HEAD_DIM)

SUBLANES = 8
LANES = 128
ROW_CHUNK = 64

V7X_VMEM_BYTES = 64 * 1024 * 1024
VMEM_LIMIT_CAP = V7X_VMEM_BYTES - 6 * 1024 * 1024


def _params(semantics, vmem_bytes, flags=None):
    limit = int(min(max(vmem_bytes * 4 // 3, 16 * 1024 * 1024), VMEM_LIMIT_CAP))
    return pltpu.CompilerParams(dimension_semantics=semantics, vmem_limit_bytes=limit, flags=flags)


def _nbytes(shape, dtype):
    return int(np.prod(shape)) * jnp.dtype(dtype).itemsize


def _tile(n, pref):
    t = min(n, pref)
    assert n % t == 0, (n, t)
    return t


def _rmsnorm_body(x_ref, g_ref, o_ref):
    x = x_ref[...]
    ms = jnp.mean(x * x, axis=-1, keepdims=True)
    o_ref[...] = ((x * lax.rsqrt(ms + EPS)) * g_ref[...]).astype(o_ref.dtype)


def rmsnorm(x, g):
    t, d = x.shape
    tm = _tile(t, 512)
    vmem = 2 * _nbytes((tm, d), F32) + 2 * _nbytes((tm, d), BF16) + 2 * _nbytes((tm, d), F32)
    return pl.pallas_call(
        _rmsnorm_body,
        grid=(t // tm,),
        in_specs=[pl.BlockSpec((tm, d), lambda i: (i, 0)),
                  pl.BlockSpec((1, d), lambda i: (0, 0))],
        out_specs=pl.BlockSpec((tm, d), lambda i: (i, 0)),
        out_shape=jax.ShapeDtypeStruct((t, d), BF16),
        compiler_params=_params(("parallel",), vmem),
        name="rmsnorm",
    )(x, g.reshape(1, d))


def _ln_swish_body(c_ref, g_ref, b_ref, o_ref):
    def rows(k, carry):
        r0 = pl.multiple_of(k * SUBLANES, SUBLANES)
        y = c_ref[pl.ds(r0, SUBLANES), :]
        mu = jnp.mean(y, axis=-1, keepdims=True)
        yc = y - mu
        var = jnp.mean(yc * yc, axis=-1, keepdims=True)
        z = (yc * lax.rsqrt(var + EPS)) * g_ref[...] + b_ref[...]
        o_ref[pl.ds(r0, SUBLANES), :] = (z * jax.nn.sigmoid(z)).astype(o_ref.dtype)
        return carry

    lax.fori_loop(0, c_ref.shape[0] // SUBLANES, rows, 0)


def ln_swish(c, g, b):
    t, d = c.shape
    tm = _tile(t, 512)
    vmem = 2 * _nbytes((tm, d), F32) + 2 * _nbytes((tm, d), BF16)
    return pl.pallas_call(
        _ln_swish_body,
        grid=(t // tm,),
        in_specs=[pl.BlockSpec((tm, d), lambda i: (i, 0)),
                  pl.BlockSpec((1, d), lambda i: (0, 0)),
                  pl.BlockSpec((1, d), lambda i: (0, 0))],
        out_specs=pl.BlockSpec((tm, d), lambda i: (i, 0)),
        out_shape=jax.ShapeDtypeStruct((t, d), BF16),
        compiler_params=_params(("parallel",), vmem),
        name="ln_swish",
    )(c, g.reshape(1, d), b.reshape(1, d))


def _wres_body(*refs, n_w, n_x, n_rows, n_steps, tn, epilogue):
    a_ref = refs[0]
    w_refs = refs[1:1 + n_w]
    x_refs = refs[1 + n_w:1 + n_w + n_x]
    o_ref = refs[1 + n_w + n_x]
    wb_ref, r_ref = refs[2 + n_w + n_x:4 + n_w + n_x]
    s_refs = refs[4 + n_w + n_x:]

    s = pl.program_id(0)
    i = lax.rem(jnp.minimum(s, n_steps - 1), n_rows)
    e = jnp.maximum(s - 1, 0)
    i_e = lax.rem(e, n_rows)
    j_e = lax.div(e, n_rows)

    @pl.when(s == 0)
    def _():
        r_ref[1] = jnp.zeros(r_ref.shape[1:], F32)
        for s_ref in s_refs:
            s_ref[...] = jnp.zeros(s_ref.shape, s_ref.dtype)

    @pl.when(jnp.logical_and(i == 0, s < n_steps))
    def _():
        for k, w_ref in enumerate(w_refs):
            wb_ref[:, k * tn:(k + 1) * tn] = w_ref[...].astype(BF16)

    slot = lax.rem(s, 2)
    r_ref[slot] = jnp.dot(a_ref[...], wb_ref[...], preferred_element_type=F32)
    epilogue(r_ref.at[1 - slot], x_refs, o_ref, s_refs, i_e, j_e)


def _wres_matmul(a, w_ops, extras, epilogue, *, tm, tn, out_cols, out_tn, out_dtype, scratch, name,
                 w_buffers=2, flags=None):
    t, k = a.shape
    n_rows = t // tm
    n_cols = out_cols // out_tn
    n_steps = n_rows * n_cols
    n_tot = len(w_ops) * tn

    def cur(s):
        sc = jnp.minimum(s, n_steps - 1)
        return lax.rem(sc, n_rows), lax.div(sc, n_rows)

    def prev(s):
        e = jnp.maximum(s - 1, 0)
        return lax.rem(e, n_rows), lax.div(e, n_rows)

    in_specs = [pl.BlockSpec((tm, k), lambda s: (cur(s)[0], 0))]
    args = [a]
    for w, layer, off in w_ops:
        assert w.shape[1] == k
        in_specs.append(pl.BlockSpec((None, k, tn), lambda s, layer=layer, off=off: (layer, 0, off + cur(s)[1]),
                                     pipeline_mode=pl.Buffered(w_buffers)))
        args.append(w)
    vmem = 2 * _nbytes((tm, k), BF16) + w_buffers * len(w_ops) * _nbytes((k, tn), F32)
    for arr, shape, fn in extras:
        in_specs.append(pl.BlockSpec(shape, lambda s, fn=fn: fn(*prev(s))))
        args.append(arr)
        vmem += 2 * _nbytes([d for d in shape if d is not None], arr.dtype)
    scratch_shapes = [pltpu.VMEM((k, n_tot), BF16), pltpu.VMEM((2, tm, n_tot), F32)]
    scratch_shapes += [pltpu.VMEM(shape, dtype) for shape, dtype in scratch]
    vmem += _nbytes((k, n_tot), BF16) + 3 * _nbytes((tm, n_tot), F32) + 2 * _nbytes((tm, out_tn), out_dtype)
    vmem += sum(_nbytes(shape, dtype) for shape, dtype in scratch)
    body = functools.partial(_wres_body, n_w=len(w_ops), n_x=len(extras), n_rows=n_rows, n_steps=n_steps,
                             tn=tn, epilogue=epilogue)
    return pl.pallas_call(
        body,
        grid=(n_steps + 1,),
        in_specs=in_specs,
        out_specs=pl.BlockSpec((tm, out_tn), lambda s: prev(s)),
        out_shape=jax.ShapeDtypeStruct((t, out_cols), out_dtype),
        scratch_shapes=scratch_shapes,
        compiler_params=_params(("arbitrary",), vmem, flags),
        name=name,
    )(*args)


def _shift_rows(x, r):
    return x if r == 0 else pltpu.roll(x, r, 0)


def _pw1_conv_epilogue(r_prev, x_refs, o_ref, s_refs, i_e, j_e, *, tm, tn):
    ba_ref, bg_ref, cw_ref, cb_ref = x_refs
    (u_ref,) = s_refs
    halo = u_ref[tm:tm + CONV_HALO, :]
    u_ref[0:CONV_HALO, :] = jnp.where(i_e == 0, jnp.zeros_like(halo), halo)
    for c in range(tm // ROW_CHUNK):
        rows = slice(c * ROW_CHUNK, (c + 1) * ROW_CHUNK)
        a = r_prev[rows, 0:tn] + ba_ref[...]
        g = r_prev[rows, tn:2 * tn] + bg_ref[...]
        u_ref[CONV_HALO + c * ROW_CHUNK:CONV_HALO + (c + 1) * ROW_CHUNK, :] = a * jax.nn.sigmoid(g)

    span = ROW_CHUNK + SUBLANES
    base = CONV_HALO - SUBLANES
    n_a = -(-CONV_WIDTH // SUBLANES)
    for c in range(tm // ROW_CHUNK):
        t0 = c * ROW_CHUNK
        for l in range(tn // LANES):
            lanes = slice(l * LANES, (l + 1) * LANES)
            wins = [u_ref[t0 + base - SUBLANES * a:t0 + base - SUBLANES * a + span, lanes] for a in range(n_a)]
            acc = None
            for r in range(SUBLANES):
                z = None
                for a in range(n_a):
                    k = CONV_WIDTH - 1 - SUBLANES * a - r
                    if k < 0:
                        continue
                    term = wins[a] * cw_ref[k:k + 1, lanes]
                    z = term if z is None else z + term
                z = _shift_rows(z, r)[SUBLANES:]
                acc = z if acc is None else acc + z
            o_ref[t0:t0 + ROW_CHUNK, lanes] = acc + cb_ref[:, lanes]


def pw1_glu_conv(h, w, layer, b, dw_w, dw_b):
    t, d = h.shape
    tm = _tile(t, 1024)
    tn = _tile(d, 256)
    half = d // tn
    b2 = b.reshape(1, 2 * d)
    extras = [(b2, (1, tn), lambda i, j: (0, j)),
              (b2, (1, tn), lambda i, j: (0, j + half)),
              (dw_w, (CONV_WIDTH, tn), lambda i, j: (0, j)),
              (dw_b.reshape(1, d), (1, tn), lambda i, j: (0, j))]
    return _wres_matmul(
        h, [(w, layer, 0), (w, layer, half)], extras,
        functools.partial(_pw1_conv_epilogue, tm=tm, tn=tn),
        tm=tm, tn=tn, out_cols=d, out_tn=tn, out_dtype=F32,
        scratch=[((CONV_HALO + tm, tn), F32)], name="pw1_glu_conv")


def _ffn_in_epilogue(r_prev, x_refs, o_ref, s_refs, i_e, j_e, *, tm, tn):
    cw_ref, cb_ref = x_refs
    (halo_ref,) = s_refs
    halo = halo_ref[...]
    halo = jnp.where(i_e == 0, jnp.zeros_like(halo), halo)
    for c in range(tm // ROW_CHUNK):
        t0 = c * ROW_CHUNK
        for l in range(tn // LANES):
            lanes = slice(l * LANES, (l + 1) * LANES)
            glanes = slice(l * LANES, (l + 1) * LANES)
            vlanes = slice(tn + l * LANES, tn + (l + 1) * LANES)
            if c == 0:
                win = jnp.concatenate([halo[:, lanes], r_prev[0:ROW_CHUNK, glanes]], axis=0)
            else:
                win = r_prev[t0 - SUBLANES:t0 + ROW_CHUNK, glanes]
            conv = None
            for k in range(FFN_CONV_WIDTH):
                term = _shift_rows(win, FFN_CONV_WIDTH - 1 - k)[SUBLANES:] * cw_ref[k:k + 1, lanes]
                conv = term if conv is None else conv + term
            conv = conv + cb_ref[:, lanes]
            val = r_prev[t0:t0 + ROW_CHUNK, vlanes]
            o_ref[t0:t0 + ROW_CHUNK, lanes] = ((conv * jax.nn.sigmoid(conv)) * val).astype(o_ref.dtype)
    halo_ref[...] = r_prev[tm - SUBLANES:tm, 0:tn]


def ffn_in(h, w_in, layer, dw_w, dw_b):
    t, d = h.shape
    f = w_in.shape[2] // 2
    tm = _tile(t, 1024)
    tn = _tile(f, 256)
    half = f // tn
    extras = [(dw_w, (None, FFN_CONV_WIDTH, tn), lambda i, j: (layer, 0, j)),
              (dw_b.reshape(-1, 1, f), (None, 1, tn), lambda i, j: (layer, 0, j))]
    return _wres_matmul(
        h, [(w_in, layer, 0), (w_in, layer, half)], extras,
        functools.partial(_ffn_in_epilogue, tm=tm, tn=tn),
        tm=tm, tn=tn, out_cols=f, out_tn=tn, out_dtype=BF16,
        scratch=[((SUBLANES, tn), F32)], name="ffn_in")


def _qkv_epilogue(r_prev, x_refs, o_ref, s_refs, i_e, j_e, *, tm, tn, q_blocks, qk_blocks):
    qg_ref, kg_ref = x_refs
    g = jnp.where(j_e < q_blocks, qg_ref[...], kg_ref[...])
    is_qk = j_e < qk_blocks
    for c in range(tm // ROW_CHUNK):
        rows = slice(c * ROW_CHUNK, (c + 1) * ROW_CHUNK)
        for hh in range(tn // HEAD_DIM):
            lanes = slice(hh * HEAD_DIM, (hh + 1) * HEAD_DIM)
            blk = r_prev[rows, lanes]
            ms = jnp.mean(blk * blk, axis=-1, keepdims=True)
            normed = (blk * lax.rsqrt(ms + EPS)) * g
            o_ref[rows, lanes] = jnp.where(is_qk, normed, blk).astype(o_ref.dtype)


def qkv_proj(h, w, layer, q_g, k_g, n_heads, n_kv):
    t, d = h.shape
    n = w.shape[2]
    tm = _tile(t, 1024)
    tn = _tile(n_kv * HEAD_DIM, 512)
    hpb = tn // HEAD_DIM
    extras = [(q_g.reshape(1, HEAD_DIM), (1, HEAD_DIM), lambda i, j: (0, 0)),
              (k_g.reshape(1, HEAD_DIM), (1, HEAD_DIM), lambda i, j: (0, 0))]
    return _wres_matmul(
        h, [(w, layer, 0)], extras,
        functools.partial(_qkv_epilogue, tm=tm, tn=tn, q_blocks=n_heads // hpb, qk_blocks=(n_heads + n_kv) // hpb),
        tm=tm, tn=tn, out_cols=n, out_tn=tn, out_dtype=BF16, scratch=[], name="qkv_proj")


def _res_bias_epilogue(r_prev, x_refs, o_ref, s_refs, i_e, j_e):
    b_ref, x_ref = x_refs
    o_ref[...] = x_ref[...] + (r_prev[...] + b_ref[...])


def _res_epilogue(r_prev, x_refs, o_ref, s_refs, i_e, j_e):
    (x_ref,) = x_refs
    o_ref[...] = x_ref[...] + r_prev[...]


def proj_residual_wres(a, w, layer, x, b=None, *, name):
    t, k = a.shape
    n = w.shape[2]
    tm = _tile(t, 1024)
    tn = _tile(n, 512)
    x_extra = (x, (tm, tn), lambda i, j: (i, j))
    if b is None:
        extras, epilogue = [x_extra], _res_epilogue
    else:
        extras, epilogue = [(b.reshape(1, n), (1, tn), lambda i, j: (0, j)), x_extra], _res_bias_epilogue
    return _wres_matmul(a, [(w, layer, 0)], extras, epilogue, tm=tm, tn=tn, out_cols=n, out_tn=tn,
                        out_dtype=F32, scratch=[], name=name)


def _proj_res_body(a_ref, w_ref, x_ref, o_ref):
    y = jnp.dot(a_ref[...], w_ref[...], preferred_element_type=F32)
    o_ref[...] = x_ref[...] + y


def proj_residual(a, w, x, *, tm_pref, tn_pref, name):
    t, k = a.shape
    n = w.shape[1]
    tm = _tile(t, tm_pref)
    tn = _tile(n, tn_pref)
    vmem = (2 * _nbytes((tm, k), BF16) + 2 * _nbytes((k, tn), BF16) + 4 * _nbytes((tm, tn), F32)
            + 2 * _nbytes((tm, tn), F32))
    x_spec = pl.BlockSpec((tm, tn), lambda i, j: (i, j))
    return pl.pallas_call(
        _proj_res_body,
        grid=(t // tm, n // tn),
        in_specs=[pl.BlockSpec((tm, k), lambda i, j: (i, 0)),
                  pl.BlockSpec((k, tn), lambda i, j: (0, j)),
                  x_spec],
        out_specs=x_spec,
        out_shape=jax.ShapeDtypeStruct((t, n), F32),
        compiler_params=_params(("parallel", "arbitrary"), vmem),
        name=name,
    )(a, w, x)


def _bucket_index():
    qi = np.arange(BLOCK)[:, None]
    kj = np.arange(2 * BLOCK)[None, :]
    dist = qi - kj + BLOCK
    n = np.maximum(dist, 0)
    nf = np.maximum(n, 1).astype(np.float64)
    large = MAX_EXACT + (np.log(nf / MAX_EXACT) / math.log(MAX_DISTANCE / MAX_EXACT)
                         * (NUM_BUCKETS - MAX_EXACT)).astype(np.int64)
    large = np.minimum(large, NUM_BUCKETS - 1)
    bucket = np.where(n < MAX_EXACT, n, large)
    in_window = (dist >= 0) & (dist < WINDOW)
    return np.where(in_window, bucket, -1).astype(np.int32)


def _bias_body(rb_ref, idx_ref, o_ref):
    h = pl.program_id(0)
    idx = idx_ref[...]
    acc = jnp.full(idx.shape, -jnp.inf, F32)
    for b in range(NUM_BUCKETS):
        acc = jnp.where(idx == b, rb_ref[b, h], acc)
    o_ref[0] = acc


def bias_table(rel_bias):
    n_heads = rel_bias.shape[1]
    idx = jnp.asarray(_bucket_index())
    return pl.pallas_call(
        _bias_body,
        grid=(n_heads,),
        in_specs=[pl.BlockSpec(memory_space=pltpu.SMEM),
                  pl.BlockSpec((BLOCK, 2 * BLOCK), lambda h: (0, 0))],
        out_specs=pl.BlockSpec((1, BLOCK, 2 * BLOCK), lambda h: (h, 0, 0)),
        out_shape=jax.ShapeDtypeStruct((n_heads, BLOCK, 2 * BLOCK), F32),
        compiler_params=_params(("arbitrary",), 0),
        name="bias_table",
    )(rel_bias, idx)


def _attn_body(sink_ref, q_ref, kc_ref, kp_ref, vc_ref, vp_ref, bias_ref, o_ref, *, n_kv):
    n = pl.program_id(0)
    rows = GROUP * BLOCK
    col = lax.broadcasted_iota(jnp.int32, (rows, 2 * BLOCK), 1)
    no_prev = jnp.logical_and(n == 0, col < BLOCK)
    for kv in range(n_kv):
        ksl = slice(kv * HEAD_DIM, (kv + 1) * HEAD_DIM)
        k = jnp.concatenate([kp_ref[:, ksl], kc_ref[:, ksl]], axis=0)
        v = jnp.concatenate([vp_ref[:, ksl], vc_ref[:, ksl]], axis=0)
        heads = [kv * GROUP + g for g in range(GROUP)]
        q = jnp.concatenate([q_ref[:, h * HEAD_DIM:(h + 1) * HEAD_DIM] for h in heads], axis=0)
        s = lax.dot_general(q, k, (((1,), (1,)), ((), ())), preferred_element_type=F32)
        bias = bias_ref[kv * GROUP:(kv + 1) * GROUP].reshape(rows, 2 * BLOCK)
        s = s * ATTN_SCALE + bias
        s = jnp.where(no_prev, -jnp.inf, s)
        sink = jnp.concatenate([jnp.full((BLOCK, 1), sink_ref[h], F32) for h in heads], axis=0)
        m = jnp.maximum(jnp.max(s, axis=-1, keepdims=True), sink)
        p = jnp.exp(s - m)
        denom = jnp.sum(p, axis=-1, keepdims=True) + jnp.exp(sink - m)
        p = (p / denom).astype(v.dtype)
        o = jnp.dot(p, v, preferred_element_type=F32)
        for g, h in enumerate(heads):
            o_ref[:, h * HEAD_DIM:(h + 1) * HEAD_DIM] = o[g * BLOCK:(g + 1) * BLOCK].astype(o_ref.dtype)


def attention(qkv, bias, sinks, n_heads, n_kv):
    t = qkv.shape[0]
    dq = n_heads * HEAD_DIM
    dkv = n_kv * HEAD_DIM
    k_blk = dq // dkv
    v_blk = k_blk + 1
    prev = lambda n: jnp.maximum(n - 1, 0)
    vmem = (2 * _nbytes((BLOCK, dq), BF16) * 2 + 8 * _nbytes((BLOCK, dkv), BF16)
            + 2 * _nbytes(bias.shape, F32) + 16 * _nbytes((GROUP * BLOCK, 2 * BLOCK), F32))
    return pl.pallas_call(
        functools.partial(_attn_body, n_kv=n_kv),
        grid=(t // BLOCK,),
        in_specs=[pl.BlockSpec(memory_space=pltpu.SMEM),
                  pl.BlockSpec((BLOCK, dq), lambda n: (n, 0)),
                  pl.BlockSpec((BLOCK, dkv), lambda n: (n, k_blk)),
                  pl.BlockSpec((BLOCK, dkv), lambda n: (prev(n), k_blk)),
                  pl.BlockSpec((BLOCK, dkv), lambda n: (n, v_blk)),
                  pl.BlockSpec((BLOCK, dkv), lambda n: (prev(n), v_blk)),
                  pl.BlockSpec(bias.shape, lambda n: (0, 0, 0))],
        out_specs=pl.BlockSpec((BLOCK, dq), lambda n: (n, 0)),
        out_shape=jax.ShapeDtypeStruct((t, dq), BF16),
        compiler_params=_params(("parallel",), vmem),
        name="swa_attention",
    )(sinks, qkv, qkv, qkv, qkv, qkv, bias)


def _ffn_block(x, layer, norm_g, w_in, dw_w, dw_b, w_out, name):
    h = rmsnorm(x, norm_g[layer])
    act = ffn_in(h, w_in, layer, dw_w, dw_b)
    return proj_residual(act, w_out[layer].astype(BF16), x, tm_pref=512, tn_pref=512, name=name)


def kernel(x, mix_norm_g, conv_pw1_w, conv_pw1_b, conv_dw_w, conv_dw_b, conv_ln_g, conv_ln_b, conv_pw2_w, conv_pw2_b, attn_w_qkv, attn_q_norm_g, attn_k_norm_g, attn_sinks, attn_w_o, rel_bias, ffn_norm_g, ffn_w_in, ffn_dw_w, ffn_dw_b, ffn_w_out):
    batch, t, d = x.shape
    assert batch == 1
    n_heads = attn_sinks.shape[1]
    n_kv = n_heads // GROUP
    assert attn_w_qkv.shape[2] == (n_heads + 2 * n_kv) * HEAD_DIM and t % BLOCK == 0
    x = x.reshape(t, d)

    h = rmsnorm(x, mix_norm_g[0])
    c = pw1_glu_conv(h, conv_pw1_w, 0, conv_pw1_b[0], conv_dw_w[0], conv_dw_b[0])
    v = ln_swish(c, conv_ln_g[0], conv_ln_b[0])
    x = proj_residual_wres(v, conv_pw2_w, 0, x, conv_pw2_b[0], name="pw2_res")
    x = _ffn_block(x, 0, ffn_norm_g, ffn_w_in, ffn_dw_w, ffn_dw_b, ffn_w_out, "ffn_out0")

    h = rmsnorm(x, mix_norm_g[1])
    qkv = qkv_proj(h, attn_w_qkv, 0, attn_q_norm_g[0], attn_k_norm_g[0], n_heads, n_kv)
    o = attention(qkv, bias_table(rel_bias), attn_sinks[0], n_heads, n_kv)
    x = proj_residual_wres(o, attn_w_o, 0, x, name="wo_res")
    x = _ffn_block(x, 1, ffn_norm_g, ffn_w_in, ffn_dw_w, ffn_dw_b, ffn_w_out, "ffn_out1")
    return x.reshape(batch, t, d)
```
